```python
import math
import jax, jax.numpy as jnp
from jax import lax
import numpy as np

D_MODEL = 2048
BATCH = 1
SEQ = 16384
DEPTH = 2

SB_HEADS = 8
SB_HEAD_DIM = D_MODEL // 32
RET_HEADS = 8
RET_HEAD_DIM = D_MODEL // 16
SGU_GROUPS = 4
SGU_GROUP_DIM = D_MODEL // 16
SB_W = SB_HEADS * SB_HEAD_DIM
RET_W = RET_HEADS * RET_HEAD_DIM
SGU_W = SGU_GROUPS * SGU_GROUP_DIM
MIX_W = SB_W + RET_W + SGU_W
IN_COLS = 3 * SB_W + 4 * RET_W + 2 * SGU_W

SB_BLOCK = 128
RET_CHUNK = 128
RET_DECAY_BASE = 5.0
ROPE_BASE = 10000.0
SGU_CHUNK = 128

PEER_HEADS = 8
PEER_NKEYS = 128
PEER_EXPERTS = PEER_NKEYS * PEER_NKEYS
PEER_TOPK = 16
PEER_QDIM = 256
PEER_SUBDIM = PEER_QDIM // 2
PEER_TOKEN_CHUNK = 128

DN_ALPHA = (2.0 * DEPTH) ** 0.25
DN_BETA = (8.0 * DEPTH) ** -0.25
LN_EPS = 1e-5

kernel_name = 'hymba_style_sb_ret_sgu_peer_deepnorm'


def _layernorm(x, g, b):
    xf = x.astype(jnp.float32)
    mu = jnp.mean(xf, axis=-1, keepdims=True)
    var = jnp.mean(jnp.square(xf - mu), axis=-1, keepdims=True)
    return ((xf - mu) * lax.rsqrt(var + LN_EPS) * g.astype(jnp.float32) + b.astype(jnp.float32)).astype(x.dtype)


def _stick_breaking(q, k, v):
    b_, s_, h_, dh = q.shape
    nb = s_ // SB_BLOCK
    scale = dh ** -0.5
    qb = q.reshape(b_, nb, SB_BLOCK, h_, dh).transpose(1, 0, 2, 3, 4)
    kf = k.astype(jnp.float32)
    vf = v.astype(jnp.float32)
    key_pos = jnp.arange(s_)

    def one_block(args):
        qblk, bi = args
        z = jnp.einsum('bqhd,bkhd->bhqk', qblk.astype(jnp.float32), kf) * scale
        q_pos = bi * SB_BLOCK + jnp.arange(SB_BLOCK)
        mask = key_pos[None, :] < q_pos[:, None]
        log_beta = jax.nn.log_sigmoid(z)
        log_1m = jnp.where(mask, log_beta - z, 0.0)
        between = lax.cumsum(log_1m, axis=3, reverse=True) - log_1m
        a = jnp.where(mask, jnp.exp(log_beta + between), 0.0)
        return jnp.einsum('bhqk,bkhd->bqhd', a, vf)

    o = lax.map(one_block, (qb, jnp.arange(nb)))
    return o.transpose(1, 0, 2, 3, 4).reshape(b_, s_, h_ * dh).astype(q.dtype)


def _rotary(x, pos):
    half = x.shape[-1] // 2
    inv = ROPE_BASE ** (-jnp.arange(half, dtype=jnp.float32) / half)
    ang = pos.astype(jnp.float32)[:, None] * inv[None, :]
    cos = jnp.cos(ang)[None, :, None, :]
    sin = jnp.sin(ang)[None, :, None, :]
    x1 = x[..., :half].astype(jnp.float32)
    x2 = x[..., half:].astype(jnp.float32)
    return jnp.concatenate([x1 * cos - x2 * sin, x1 * sin + x2 * cos], axis=-1)


def _retention(q, k, v, g):
    b_, s_, h_, d = q.shape
    c_ = RET_CHUNK
    n = s_ // c_
    pos = jnp.arange(s_)
    qf = _rotary(q, pos)
    kf = _rotary(k, pos) * (d ** -0.5)
    vf = v.astype(jnp.float32)
    log_gamma = jnp.log1p(-jnp.exp2(-RET_DECAY_BASE - jnp.arange(h_, dtype=jnp.float32)))
    idx = jnp.arange(c_, dtype=jnp.float32)
    diff = idx[:, None] - idx[None, :]
    decay = jnp.where(diff >= 0, jnp.exp(log_gamma[:, None, None] * jnp.maximum(diff, 0.0)), 0.0)
    qc = qf.reshape(b_, n, c_, h_, d)
    kc = kf.reshape(b_, n, c_, h_, d)
    vc = vf.reshape(b_, n, c_, h_, d)
    scores = jnp.einsum('bnchd,bnshd->bnhcs', qc, kc) * decay
    y_inner = jnp.einsum('bnhcs,bnshe->bnche', scores, vc)
    zeta = jnp.exp(log_gamma[:, None] * (c_ - 1.0 - idx)[None, :])
    kv = jnp.einsum('bnshd,bnshe,hs->nbhde', kc, vc, zeta)
    chunk_decay = jnp.exp(log_gamma * c_)[None, :, None, None]

    def step(state, kv_i):
        return chunk_decay * state + kv_i, state

    _, prev = lax.scan(step, jnp.zeros((b_, h_, d, d), jnp.float32), kv)
    xi = jnp.exp(log_gamma[:, None] * (idx + 1.0)[None, :])
    y_cross = jnp.einsum('bnchd,nbhde,hc->bnche', qc, prev, xi)
    y = y_inner + y_cross
    mu = jnp.mean(y, axis=-1, keepdims=True)
    var = jnp.mean(jnp.square(y - mu), axis=-1, keepdims=True)
    y = ((y - mu) * lax.rsqrt(var + LN_EPS)).reshape(b_, s_, h_ * d)
    return (jax.nn.silu(g.astype(jnp.float32).reshape(b_, s_, h_ * d)) * y).astype(v.dtype)


def _sgu(u, v, ln_g, ln_b, w_s, b_s):
    b_, s_, _ = u.shape
    n = s_ // SGU_CHUNK
    gq, c = SGU_GROUPS, SGU_GROUP_DIM
    u = jax.nn.gelu(u)
    v = jax.nn.gelu(v)
    vg = _layernorm(v.reshape(b_, s_, gq, c), ln_g.reshape(gq, c), ln_b.reshape(gq, c))
    vg = vg.reshape(b_, n, SGU_CHUNK, gq, c)
    w = jnp.tril(w_s)
    mixed = jnp.einsum('gts,bnsgc->bntgc', w, vg) + b_s.T[None, None, :, :, None]
    return u * mixed.reshape(b_, s_, gq * c)


def _peer(x, w_q, sub_keys, u_tab, v_tab):
    b_, s_, d = x.shape
    xt = x.reshape((b_ * s_) // PEER_TOKEN_CHUNK, PEER_TOKEN_CHUNK, d)

    def one_chunk(xc):
        t = xc.shape[0]
        q = (xc @ w_q).reshape(t, PEER_HEADS, 2, PEER_SUBDIM).astype(jnp.float32)
        s = jnp.einsum('thpd,pkd->thpk', q, sub_keys.astype(jnp.float32))
        s_top, i_top = lax.top_k(s, PEER_TOPK)
        cand = s_top[:, :, 0, :, None] + s_top[:, :, 1, None, :]
        best, flat = lax.top_k(cand.reshape(t, PEER_HEADS, PEER_TOPK * PEER_TOPK), PEER_TOPK)
        ia = jnp.take_along_axis(i_top[:, :, 0, :], flat // PEER_TOPK, axis=-1)
        ib = jnp.take_along_axis(i_top[:, :, 1, :], flat % PEER_TOPK, axis=-1)
        expert = (ia * PEER_NKEYS + ib).reshape(t, PEER_HEADS * PEER_TOPK)
        gate = jax.nn.softmax(best, axis=-1).reshape(t, PEER_HEADS * PEER_TOPK)
        u_sel = u_tab[expert]
        hid = jax.nn.gelu(jnp.einsum('ted,td->te', u_sel, xc).astype(jnp.float32))
        coef = (gate * hid).astype(x.dtype)
        return jnp.einsum('te,ted->td', coef, v_tab[expert])

    return lax.map(one_chunk, xt).reshape(b_, s_, d)


def setup_inputs(seed: int = 0) -> dict:
    key = jax.random.key(seed)
    ks = jax.random.split(key, 16)
    L, D = DEPTH, D_MODEL
    f32 = jnp.float32

    def nrm(k, shape, scale):
        return jax.random.normal(k, shape, f32) * scale

    return {
        'x': nrm(ks[0], (BATCH, SEQ, D), 1.0),
        'w_in': nrm(ks[1], (L, D, IN_COLS), D ** -0.5),
        'w_out': nrm(ks[2], (L, MIX_W, D), DN_BETA * MIX_W ** -0.5),
        'sgu_ln_g': 1.0 + nrm(ks[3], (L, SGU_W), 0.02),
        'sgu_ln_b': nrm(ks[4], (L, SGU_W), 0.02),
        'sgu_w': nrm(ks[5], (L, SGU_GROUPS, SGU_CHUNK, SGU_CHUNK), SGU_CHUNK ** -0.5),
        'sgu_b': 1.0 + nrm(ks[6], (L, SGU_GROUPS, SGU_CHUNK), 0.02),
        'ln1_g': 1.0 + nrm(ks[7], (L, D), 0.02),
        'ln1_b': nrm(ks[8], (L, D), 0.02),
        'peer_wq': nrm(ks[9], (L, D, PEER_HEADS * PEER_QDIM), D ** -0.5),
        'peer_sub_keys': nrm(ks[10], (L, 2, PEER_NKEYS, PEER_SUBDIM), PEER_SUBDIM ** -0.5),
        'peer_u': nrm(ks[11], (L, PEER_EXPERTS, D), D ** -0.5),
        'peer_v': nrm(ks[12], (L, PEER_EXPERTS, D), DN_BETA * PEER_HEADS ** -0.5),
        'ln2_g': 1.0 + nrm(ks[13], (L, D), 0.02),
        'ln2_b': nrm(ks[14], (L, D), 0.02),
    }


def reference(x, w_in, w_out, sgu_ln_g, sgu_ln_b, sgu_w, sgu_b, ln1_g, ln1_b,
              peer_wq, peer_sub_keys, peer_u, peer_v, ln2_g, ln2_b):
    b_, s_, _ = x.shape
    sizes = (SB_W,) * 3 + (RET_W,) * 4 + (SGU_W,) * 2
    points = [sum(sizes[:i + 1]) for i in range(len(sizes) - 1)]
    for l in range(DEPTH):
        h = x @ w_in[l]
        sq, sk, sv, rq, rk, rv, rg, cu, cv = jnp.split(h, points, axis=-1)
        sb_shape = (b_, s_, SB_HEADS, SB_HEAD_DIM)
        ret_shape = (b_, s_, RET_HEADS, RET_HEAD_DIM)
        a_out = _stick_breaking(sq.reshape(sb_shape), sk.reshape(sb_shape), sv.reshape(sb_shape))
        r_out = _retention(rq.reshape(ret_shape), rk.reshape(ret_shape), rv.reshape(ret_shape), rg.reshape(ret_shape))
        c_out = _sgu(cu, cv, sgu_ln_g[l], sgu_ln_b[l], sgu_w[l], sgu_b[l])
        mix = jnp.concatenate([a_out, r_out.astype(x.dtype), c_out.astype(x.dtype)], axis=-1) @ w_out[l]
        x = _layernorm(DN_ALPHA * x + mix, ln1_g[l], ln1_b[l])
        ffn = _peer(x, peer_wq[l], peer_sub_keys[l], peer_u[l], peer_v[l])
        x = _layernorm(DN_ALPHA * x + ffn, ln2_g[l], ln2_b[l])
    return x
```

```python
import functools

import numpy as np
import jax
import jax.numpy as jnp
from jax import lax
from jax.experimental import pallas as pl
from jax.experimental.pallas import tpu as pltpu

D_MODEL = 2048
DEPTH = 2

SB_HEADS = 8
SB_HEAD_DIM = 64
RET_HEADS = 8
RET_HEAD_DIM = 128
SGU_GROUPS = 4
SGU_GROUP_DIM = 128
SB_W = SB_HEADS * SB_HEAD_DIM
RET_W = RET_HEADS * RET_HEAD_DIM
SGU_W = SGU_GROUPS * SGU_GROUP_DIM
IN_COLS = 3 * SB_W + 4 * RET_W + 2 * SGU_W

CHUNK = 128
RET_DECAY_BASE = 5.0
ROPE_BASE = 10000.0

PEER_HEADS = 8
PEER_NKEYS = 128
PEER_EXPERTS = PEER_NKEYS * PEER_NKEYS
PEER_TOPK = 16
PEER_SUBDIM = 128

DN_ALPHA = (2.0 * DEPTH) ** 0.25
LN_EPS = 1e-5

SB_DONE = 104.0

LANES = 128
VMEM_LIMIT = 56 * 1024 * 1024

F32 = jnp.float32
BF16 = jnp.bfloat16

_NT = (((1,), (1,)), ((), ()))


def _params(*sem):
    return pltpu.CompilerParams(dimension_semantics=sem, vmem_limit_bytes=VMEM_LIMIT)


def _layernorm_rows(y, g, b):
    mu = jnp.mean(y, axis=-1, keepdims=True)
    yc = y - mu
    var = jnp.mean(yc * yc, axis=-1, keepdims=True)
    return yc * lax.rsqrt(var + LN_EPS) * g + b


def _mm_kernel(a_ref, b_ref, o_ref):
    o_ref[...] = jnp.dot(a_ref[...], b_ref[...], preferred_element_type=F32)


def _matmul(a, b, tm, tn):
    m, k = a.shape
    n = b.shape[1]
    return pl.pallas_call(
        _mm_kernel,
        grid=(m // tm, n // tn),
        in_specs=[pl.BlockSpec((tm, k), lambda i, j: (i, 0)),
                  pl.BlockSpec((k, tn), lambda i, j: (0, j))],
        out_specs=pl.BlockSpec((tm, tn), lambda i, j: (i, j)),
        out_shape=jax.ShapeDtypeStruct((m, n), F32),
        compiler_params=_params("parallel", "parallel"),
    )(a, b)


def _attn_kernel(q_ref, h_any, tri_ref, o_ref, kbuf, vbuf, qs, carry_ref, sem):
    i = pl.program_id(0)
    qs[...] = (q_ref[...] * (SB_HEAD_DIM ** -0.5)).astype(BF16)
    carry_ref[...] = jnp.zeros_like(carry_ref)
    o_ref[...] = jnp.zeros_like(o_ref)
    tri = tri_ref[...]
    row = lax.broadcasted_iota(jnp.int32, (CHUNK, CHUNK), 0) + i * CHUNK
    col0 = lax.broadcasted_iota(jnp.int32, (CHUNK, CHUNK), 1)

    def key_copies(jb):
        r0 = pl.multiple_of(jb * CHUNK, CHUNK)
        ck = pltpu.make_async_copy(h_any.at[pl.ds(r0, CHUNK), pl.ds(SB_W, SB_W)], kbuf, sem.at[0])
        cv = pltpu.make_async_copy(h_any.at[pl.ds(r0, CHUNK), pl.ds(2 * SB_W, SB_W)], vbuf, sem.at[1])
        return ck, cv

    def cond(c):
        jb, go = c
        return jnp.logical_and(jb >= 0, go)

    def body(c):
        jb, _ = c
        ck, cv = key_copies(jb)
        ck.start()
        cv.start()
        ck.wait()
        cv.wait()
        kb = kbuf[...].astype(BF16)
        vb = vbuf[...].astype(BF16)
        mask = (col0 + jb * CHUNK) < row
        top = jnp.full((1, 1), -jnp.inf, F32)
        for h in range(SB_HEADS):
            hs = slice(h * SB_HEAD_DIM, (h + 1) * SB_HEAD_DIM)
            z = lax.dot_general(qs[:, hs], kb[:, hs], _NT, preferred_element_type=F32)
            sp = jnp.maximum(z, 0.0) + jnp.log1p(jnp.exp(-jnp.abs(z)))
            l1m = jnp.where(mask, -sp, 0.0)
            p0 = l1m.astype(BF16)
            r0 = l1m - p0.astype(F32)
            p1 = r0.astype(BF16)
            p2 = (r0 - p1.astype(F32)).astype(BF16)
            cs = (jnp.dot(p0, tri, preferred_element_type=F32)
                  + jnp.dot(p1, tri, preferred_element_type=F32)
                  + jnp.dot(p2, tri, preferred_element_type=F32))
            cr = carry_ref[h]
            a = jnp.where(mask, jnp.exp(z - sp + cs[:, :CHUNK] + cr), 0.0)
            o_ref[:, hs] += jnp.dot(a.astype(BF16), vb[:, hs], preferred_element_type=F32)
            cr = cr + cs[:, CHUNK:]
            carry_ref[h] = cr
            top = jnp.maximum(top, jnp.max(cr, axis=(0, 1), keepdims=True))
        return jb - 1, top[0, 0] > -SB_DONE

    lax.while_loop(cond, body, (i, True))


def _stick_breaking(h, tri):
    s = h.shape[0]
    return pl.pallas_call(
        _attn_kernel,
        grid=(s // CHUNK,),
        in_specs=[pl.BlockSpec((CHUNK, SB_W), lambda i: (i, 0)),
                  pl.BlockSpec(memory_space=pl.ANY),
                  pl.BlockSpec((CHUNK, 2 * CHUNK), lambda i: (0, 0))],
        out_specs=pl.BlockSpec((CHUNK, SB_W), lambda i: (i, 0)),
        out_shape=jax.ShapeDtypeStruct((s, SB_W), F32),
        scratch_shapes=[pltpu.VMEM((CHUNK, SB_W), F32),
                        pltpu.VMEM((CHUNK, SB_W), F32),
                        pltpu.VMEM((CHUNK, SB_W), BF16),
                        pltpu.VMEM((SB_HEADS, CHUNK, CHUNK), F32),
                        pltpu.SemaphoreType.DMA((2,))],
        compiler_params=_params("arbitrary"),
    )(h, h, tri)


def _ret_kernel(q_ref, k_ref, v_ref, g_ref, cos_ref, sin_ref, dec_ref, zeta_ref, xi_ref, cd_ref,
                o_ref, state_ref):
    @pl.when(pl.program_id(1) == 0)
    def _():
        state_ref[...] = jnp.zeros_like(state_ref)

    half = RET_HEAD_DIM // 2
    for c in range(q_ref.shape[0] // CHUNK):
        sl = slice(c * CHUNK, (c + 1) * CHUNK)
        cs = cos_ref[sl, :]
        sn = sin_ref[sl, :]
        q = q_ref[sl, :]
        k = k_ref[sl, :]
        qf = q * cs + pltpu.roll(q, half, 1) * sn
        kf = (k * cs + pltpu.roll(k, half, 1) * sn) * (RET_HEAD_DIM ** -0.5)
        vb = v_ref[sl, :].astype(BF16)
        scores = lax.dot_general(qf.astype(BF16), kf.astype(BF16), _NT, preferred_element_type=F32) * dec_ref[0]
        st = state_ref[...]
        y = (jnp.dot(scores.astype(BF16), vb, preferred_element_type=F32)
             + jnp.dot((qf * xi_ref[0]).astype(BF16), st.astype(BF16), preferred_element_type=F32))
        kzt = (kf * zeta_ref[0]).T
        state_ref[...] = cd_ref[0] * st + jnp.dot(kzt.astype(BF16), vb, preferred_element_type=F32)
        mu = jnp.mean(y, axis=-1, keepdims=True)
        yc = y - mu
        var = jnp.mean(yc * yc, axis=-1, keepdims=True)
        g = g_ref[sl, :]
        o_ref[sl, :] = g * jax.nn.sigmoid(g) * (yc * lax.rsqrt(var + LN_EPS))


def _retention(h, consts, rows):
    s = h.shape[0]
    cos2, sin2, dec, zeta, xi, cd = consts
    base = 3 * SB_W // RET_HEAD_DIM

    def col(which):
        return lambda hh, t: (t, base + which * RET_HEADS + hh)

    head_const = pl.BlockSpec((1, CHUNK, CHUNK), lambda hh, t: (hh, 0, 0))
    return pl.pallas_call(
        _ret_kernel,
        grid=(RET_HEADS, s // rows),
        in_specs=[pl.BlockSpec((rows, RET_HEAD_DIM), col(0)),
                  pl.BlockSpec((rows, RET_HEAD_DIM), col(1)),
                  pl.BlockSpec((rows, RET_HEAD_DIM), col(2)),
                  pl.BlockSpec((rows, RET_HEAD_DIM), col(3)),
                  pl.BlockSpec((rows, RET_HEAD_DIM), lambda hh, t: (t, 0)),
                  pl.BlockSpec((rows, RET_HEAD_DIM), lambda hh, t: (t, 0)),
                  head_const, head_const, head_const, head_const],
        out_specs=pl.BlockSpec((rows, RET_HEAD_DIM), lambda hh, t: (t, hh)),
        out_shape=jax.ShapeDtypeStruct((s, RET_W), F32),
        scratch_shapes=[pltpu.VMEM((RET_HEAD_DIM, RET_HEAD_DIM), F32)],
        compiler_params=_params("parallel", "arbitrary"),
    )(h, h, h, h, cos2, sin2, dec, zeta, xi, cd)


def _retention_consts(s):
    half = RET_HEAD_DIM // 2
    inv = ROPE_BASE ** (-jnp.arange(half, dtype=F32) / half)
    ang = jnp.arange(s, dtype=F32)[:, None] * inv[None, :]
    cos, sin = jnp.cos(ang), jnp.sin(ang)
    cos2 = jnp.concatenate([cos, cos], axis=-1)
    sin2 = jnp.concatenate([-sin, sin], axis=-1)
    log_gamma = jnp.log1p(-jnp.exp2(-RET_DECAY_BASE - jnp.arange(RET_HEADS, dtype=F32)))
    idx = jnp.arange(CHUNK, dtype=F32)
    diff = idx[:, None] - idx[None, :]
    dec = jnp.where(diff >= 0, jnp.exp(log_gamma[:, None, None] * jnp.maximum(diff, 0.0)), 0.0)
    shape = (RET_HEADS, CHUNK, CHUNK)
    zeta = jnp.broadcast_to(jnp.exp(log_gamma[:, None] * (CHUNK - 1.0 - idx)[None, :])[:, :, None], shape)
    xi = jnp.broadcast_to(jnp.exp(log_gamma[:, None] * (idx + 1.0)[None, :])[:, :, None], shape)
    cd = jnp.broadcast_to(jnp.exp(log_gamma * CHUNK)[:, None, None], shape)
    return cos2, sin2, dec, zeta, xi, cd


def _sgu_kernel(u_ref, v_ref, g_ref, b_ref, w_ref, bs_ref, o_ref):
    r = lax.broadcasted_iota(jnp.int32, (CHUNK, CHUNK), 0)
    c = lax.broadcasted_iota(jnp.int32, (CHUNK, CHUNK), 1)
    causal = r >= c
    for grp in range(SGU_GROUPS):
        gs = slice(grp * SGU_GROUP_DIM, (grp + 1) * SGU_GROUP_DIM)
        w = jnp.where(causal, w_ref[grp], 0.0).astype(BF16)
        lg = g_ref[:, gs]
        lb = b_ref[:, gs]
        for ck in range(u_ref.shape[0] // CHUNK):
            sl = slice(ck * CHUNK, (ck + 1) * CHUNK)
            vg = _layernorm_rows(jax.nn.gelu(v_ref[sl, gs]), lg, lb)
            mixed = jnp.dot(w, vg.astype(BF16), preferred_element_type=F32) + bs_ref[grp]
            o_ref[sl, gs] = jax.nn.gelu(u_ref[sl, gs]) * mixed


def _sgu(h, ln_g, ln_b, w_s, bs_b, rows):
    s = h.shape[0]
    ublk = (3 * SB_W + 4 * RET_W) // SGU_W
    full = lambda shape: pl.BlockSpec(shape, lambda t: (0,) * len(shape))
    return pl.pallas_call(
        _sgu_kernel,
        grid=(s // rows,),
        in_specs=[pl.BlockSpec((rows, SGU_W), lambda t: (t, ublk)),
                  pl.BlockSpec((rows, SGU_W), lambda t: (t, ublk + 1)),
                  full((1, SGU_W)), full((1, SGU_W)),
                  full((SGU_GROUPS, CHUNK, CHUNK)), full((SGU_GROUPS, CHUNK, CHUNK))],
        out_specs=pl.BlockSpec((rows, SGU_W), lambda t: (t, 0)),
        out_shape=jax.ShapeDtypeStruct((s, SGU_W), F32),
        compiler_params=_params("parallel"),
    )(h, h, ln_g, ln_b, w_s, bs_b)


def _oproj_kernel(a_ref, r_ref, c_ref, x_ref, w_ref, g_ref, b_ref, xo_ref, xb_ref):
    mix = (jnp.dot(a_ref[...].astype(BF16), w_ref[0:SB_W, :], preferred_element_type=F32)
           + jnp.dot(r_ref[...].astype(BF16), w_ref[SB_W:SB_W + RET_W, :], preferred_element_type=F32)
           + jnp.dot(c_ref[...].astype(BF16), w_ref[SB_W + RET_W:, :], preferred_element_type=F32))
    y = _layernorm_rows(DN_ALPHA * x_ref[...] + mix, g_ref[...], b_ref[...])
    xo_ref[...] = y
    xb_ref[...] = y.astype(BF16)


def _oproj_ln(a, r, c, x, w, g, b, rows):
    s = x.shape[0]
    rowblk = lambda n: pl.BlockSpec((rows, n), lambda t: (t, 0))
    full = lambda shape: pl.BlockSpec(shape, lambda t: (0,) * len(shape))
    return pl.pallas_call(
        _oproj_kernel,
        grid=(s // rows,),
        in_specs=[rowblk(SB_W), rowblk(RET_W), rowblk(SGU_W), rowblk(D_MODEL),
                  full((D_MODEL, D_MODEL)), full((1, D_MODEL)), full((1, D_MODEL))],
        out_specs=[rowblk(D_MODEL), rowblk(D_MODEL)],
        out_shape=[jax.ShapeDtypeStruct((s, D_MODEL), F32), jax.ShapeDtypeStruct((s, D_MODEL), BF16)],
        compiler_params=_params("parallel"),
    )(a, r, c, x, w, g, b)


def _resln_kernel(x_ref, f_ref, g_ref, b_ref, xo_ref, xb_ref):
    y = _layernorm_rows(DN_ALPHA * x_ref[...] + f_ref[...], g_ref[...], b_ref[...])
    xo_ref[...] = y
    xb_ref[...] = y.astype(BF16)


def _res_ln(x, f, g, b, rows):
    s = x.shape[0]
    rowblk = pl.BlockSpec((rows, D_MODEL), lambda t: (t, 0))
    full = pl.BlockSpec((1, D_MODEL), lambda t: (0, 0))
    return pl.pallas_call(
        _resln_kernel,
        grid=(s // rows,),
        in_specs=[rowblk, rowblk, full, full],
        out_specs=[rowblk, rowblk],
        out_shape=[jax.ShapeDtypeStruct((s, D_MODEL), F32), jax.ShapeDtypeStruct((s, D_MODEL), BF16)],
        compiler_params=_params("parallel"),
    )(x, f, g, b)


_CELLS = [(i, j) for i in range(PEER_TOPK) for j in range(PEER_TOPK) if (i + 1) * (j + 1) <= PEER_TOPK]


def _topk_kernel(xb_ref, wqt_ref, keys_ref, n_ref, e1_ref, r2_ref, e2_ref, s_scr, rank_scr, v_scr):
    tt = xb_ref.shape[0]
    qt = lax.dot_general(wqt_ref[...], xb_ref[...], _NT, preferred_element_type=F32)
    kidx = lax.broadcasted_iota(jnp.int32, (PEER_NKEYS, tt), 0).astype(F32)
    for hp in range(2 * PEER_HEADS):
        h, p = divmod(hp, 2)
        s = jnp.dot(keys_ref[p], qt[hp * PEER_SUBDIM:(hp + 1) * PEER_SUBDIM, :].astype(BF16),
                    preferred_element_type=F32)
        s_scr[hp] = s
        work = s
        rank = jnp.full((PEER_NKEYS, tt), float(PEER_TOPK), F32)
        for it in range(PEER_TOPK):
            m = jnp.max(work, axis=0, keepdims=True)
            first = jnp.min(jnp.where(work == m, kidx, float(PEER_NKEYS)), axis=0, keepdims=True)
            hit = kidx == first
            rank = jnp.where(hit, float(it), rank)
            work = jnp.where(hit, -jnp.inf, work)
            v_scr[p, it, h:h + 1, :] = m
        rank_scr[hp] = rank

    v1 = [v_scr[0, i] for i in range(PEER_TOPK)]
    v2 = [v_scr[1, j] for j in range(PEER_TOPK)]
    sums = {c: v1[c[0]] + v2[c[1]] for c in _CELLS}
    ahead = {c: jnp.full((PEER_HEADS, tt), float((c[0] + 1) * (c[1] + 1) - 1), F32) for c in _CELLS}
    for x, cx in enumerate(_CELLS):
        for cy in _CELLS[x + 1:]:
            if cy[1] >= cx[1]:
                continue
            first = sums[cx] >= sums[cy]
            ahead[cy] = ahead[cy] + jnp.where(first, 1.0, 0.0)
            ahead[cx] = ahead[cx] + jnp.where(first, 0.0, 1.0)
    ex1 = [jnp.exp(v - v1[0]) for v in v1]
    ex2 = [jnp.exp(v - v2[0]) for v in v2]
    cnt = [jnp.zeros((PEER_HEADS, tt), F32) for _ in range(PEER_TOPK)]
    zsum = jnp.zeros((PEER_HEADS, tt), F32)
    for c in _CELLS:
        sel = ahead[c] < float(PEER_TOPK)
        cnt[c[0]] = cnt[c[0]] + jnp.where(sel, 1.0, 0.0)
        zsum = zsum + jnp.where(sel, ex1[c[0]] * ex2[c[1]], 0.0)
    zinv = 1.0 / zsum

    for h in range(PEER_HEADS):
        r1 = rank_scr[2 * h]
        n = jnp.zeros((PEER_NKEYS, tt), F32)
        for i in range(PEER_TOPK):
            n = n + jnp.where(r1 == float(i), cnt[i][h:h + 1, :], 0.0)
        n_ref[h] = n
        e1_ref[h] = jnp.exp(s_scr[2 * h] - v1[0][h:h + 1, :]) * zinv[h:h + 1, :]
        r2_ref[h] = rank_scr[2 * h + 1]
        e2_ref[h] = jnp.exp(s_scr[2 * h + 1] - v2[0][h:h + 1, :])


def _peer_topk(xb, wqt, keys, tt):
    s = xb.shape[0]
    out = jax.ShapeDtypeStruct((PEER_HEADS, PEER_NKEYS, s), F32)
    oblk = pl.BlockSpec((PEER_HEADS, PEER_NKEYS, tt), lambda t: (0, 0, t))
    return pl.pallas_call(
        _topk_kernel,
        grid=(s // tt,),
        in_specs=[pl.BlockSpec((tt, D_MODEL), lambda t: (t, 0)),
                  pl.BlockSpec((2 * PEER_HEADS * PEER_SUBDIM, D_MODEL), lambda t: (0, 0)),
                  pl.BlockSpec((2, PEER_NKEYS, PEER_SUBDIM), lambda t: (0, 0, 0))],
        out_specs=[oblk, oblk, oblk, oblk],
        out_shape=[out, out, out, out],
        scratch_shapes=[pltpu.VMEM((2 * PEER_HEADS, PEER_NKEYS, tt), F32),
                        pltpu.VMEM((2 * PEER_HEADS, PEER_NKEYS, tt), F32),
                        pltpu.VMEM((2, PEER_TOPK, PEER_HEADS, tt), F32)],
        compiler_params=_params("parallel"),
    )(xb, wqt, keys)


def _dense_kernel(xb_ref, u_ref, vt_ref, n_ref, e1_ref, r2_ref, e2_ref, o_ref, acc_ref, coef_ref):
    j = pl.program_id(1)

    @pl.when(j == 0)
    def _():
        acc_ref[...] = jnp.zeros_like(acc_ref)

    hid = lax.dot_general(u_ref[...], xb_ref[...], _NT, preferred_element_type=F32)
    for a in range(u_ref.shape[0] // PEER_NKEYS):
        gate = jnp.zeros((PEER_NKEYS, xb_ref.shape[0]), F32)
        for h in range(PEER_HEADS):
            gate = gate + jnp.where(r2_ref[h] < n_ref[h, a:a + 1, :], e2_ref[h] * e1_ref[h, a:a + 1, :], 0.0)
        rows = slice(a * PEER_NKEYS, (a + 1) * PEER_NKEYS)
        coef_ref[rows, :] = (gate * jax.nn.gelu(hid[rows, :])).astype(BF16)
    acc_ref[...] += jnp.dot(vt_ref[...], coef_ref[...], preferred_element_type=F32)

    @pl.when(j == pl.num_programs(1) - 1)
    def _():
        o_ref[...] = acc_ref[...].T


def _peer_dense(xb, u, vt, n, e1, r2, e2, tt, eb):
    s = xb.shape[0]
    ablk = eb // PEER_NKEYS
    per_a = pl.BlockSpec((PEER_HEADS, ablk, tt), lambda t, j: (0, j, t))
    per_b = pl.BlockSpec((PEER_HEADS, PEER_NKEYS, tt), lambda t, j: (0, 0, t))
    return pl.pallas_call(
        _dense_kernel,
        grid=(s // tt, PEER_EXPERTS // eb),
        in_specs=[pl.BlockSpec((tt, D_MODEL), lambda t, j: (t, 0)),
                  pl.BlockSpec((eb, D_MODEL), lambda t, j: (j, 0)),
                  pl.BlockSpec((D_MODEL, eb), lambda t, j: (0, j)),
                  per_a, per_a, per_b, per_b],
        out_specs=pl.BlockSpec((tt, D_MODEL), lambda t, j: (t, 0)),
        out_shape=jax.ShapeDtypeStruct((s, D_MODEL), F32),
        scratch_shapes=[pltpu.VMEM((D_MODEL, tt), F32), pltpu.VMEM((eb, tt), BF16)],
        compiler_params=_params("parallel", "arbitrary"),
    )(xb, u, vt, n, e1, r2, e2)


def _tri_matrix():
    j = np.arange(CHUNK)[:, None]
    s = np.arange(2 * CHUNK)[None, :]
    return jnp.asarray(np.where((s >= CHUNK) | (j > s), 1.0, 0.0), dtype=BF16)


def kernel(x, w_in, w_out, sgu_ln_g, sgu_ln_b, sgu_w, sgu_b, ln1_g, ln1_b, peer_wq, peer_sub_keys,
           peer_u, peer_v, ln2_g, ln2_b):
    bsz, s, d = x.shape
    assert bsz == 1 and d == D_MODEL and s % 512 == 0
    tm = min(1024, s)
    tri = _tri_matrix()
    ret_consts = _retention_consts(s)

    xf = x.reshape(s, d)
    xb = xf.astype(BF16)
    for l in range(DEPTH):
        h = _matmul(xb, w_in[l].astype(BF16), tm, 512)
        a_out = _stick_breaking(h, tri)
        r_out = _retention(h, ret_consts, 512)
        bs_b = jnp.broadcast_to(sgu_b[l][:, :, None], (SGU_GROUPS, CHUNK, CHUNK))
        c_out = _sgu(h, sgu_ln_g[l][None, :], sgu_ln_b[l][None, :], sgu_w[l], bs_b, 256)
        xf, xb = _oproj_ln(a_out, r_out, c_out, xf, w_out[l].astype(BF16),
                           ln1_g[l][None, :], ln1_b[l][None, :], 256)
        n, e1, r2, e2 = _peer_topk(xb, peer_wq[l].T.astype(BF16), peer_sub_keys[l].astype(BF16), 256)
        ffn = _peer_dense(xb, peer_u[l].astype(BF16), peer_v[l].T.astype(BF16), n, e1, r2, e2, 512, 1024)
        xf, xb = _res_ln(xf, ffn, ln2_g[l][None, :], ln2_b[l][None, :], 512)
    return xf.reshape(bsz, s, d)
```

```python
import numpy as np
import jax
import jax.numpy as jnp
from jax import lax
from jax.experimental import pallas as pl
from jax.experimental.pallas import tpu as pltpu

D_MODEL = 2048
DEPTH = 2

SB_HEADS = 8
SB_HEAD_DIM = 64
RET_HEADS = 8
RET_HEAD_DIM = 128
SGU_GROUPS = 4
SGU_GROUP_DIM = 128
SB_W = SB_HEADS * SB_HEAD_DIM
RET_W = RET_HEADS * RET_HEAD_DIM
SGU_W = SGU_GROUPS * SGU_GROUP_DIM
IN_COLS = 3 * SB_W + 4 * RET_W + 2 * SGU_W

CHUNK = 128
RET_DECAY_BASE = 5.0
ROPE_BASE = 10000.0

PEER_HEADS = 8
PEER_NKEYS = 128
PEER_EXPERTS = PEER_NKEYS * PEER_NKEYS
PEER_TOPK = 16
PEER_SUBDIM = 128

DN_ALPHA = (2.0 * DEPTH) ** 0.25
LN_EPS = 1e-5

SB_DONE = 104.0

VMEM_LIMIT = 56 * 1024 * 1024
DENSE_HALF = 512

F32 = jnp.float32
BF16 = jnp.bfloat16

_NT = (((1,), (1,)), ((), ()))


def _params(*sem):
    return pltpu.CompilerParams(dimension_semantics=sem, vmem_limit_bytes=VMEM_LIMIT)


def _layernorm_rows(y, g, b):
    mu = jnp.mean(y, axis=-1, keepdims=True)
    yc = y - mu
    var = jnp.mean(yc * yc, axis=-1, keepdims=True)
    return yc * lax.rsqrt(var + LN_EPS) * g + b


def _mm_kernel(a_ref, b_ref, o_ref):
    o_ref[...] = jnp.dot(a_ref[...], b_ref[...], preferred_element_type=F32)


def _matmul(a, b, tm, tn):
    m, k = a.shape
    n = b.shape[1]
    return pl.pallas_call(
        _mm_kernel,
        grid=(m // tm, n // tn),
        in_specs=[pl.BlockSpec((tm, k), lambda i, j: (i, 0)),
                  pl.BlockSpec((k, tn), lambda i, j: (0, j))],
        out_specs=pl.BlockSpec((tm, tn), lambda i, j: (i, j)),
        out_shape=jax.ShapeDtypeStruct((m, n), F32),
        compiler_params=_params("parallel", "parallel"),
    )(a, b)


def _attn_kernel(q_ref, kd_ref, vd_ref, h_any, tri_ref, o_ref, kbuf, vbuf, carry_ref, oacc_ref, sem):
    i = pl.program_id(0)
    q = (q_ref[...] * (SB_HEAD_DIM ** -0.5)).astype(BF16)
    tri = tri_ref[...]

    def key_copies(jb, slot):
        r0 = pl.multiple_of(jb * CHUNK, CHUNK)
        ck = pltpu.make_async_copy(h_any.at[pl.ds(r0, CHUNK), pl.ds(SB_W, SB_W)], kbuf.at[slot], sem.at[0, slot])
        cv = pltpu.make_async_copy(h_any.at[pl.ds(r0, CHUNK), pl.ds(2 * SB_W, SB_W)], vbuf.at[slot], sem.at[1, slot])
        return ck, cv

    def start(jb, slot):
        for c in key_copies(jb, slot):
            c.start()

    def wait(jb, slot):
        for c in key_copies(jb, slot):
            c.wait()

    def block(kf, vf, carry, mask):
        kb = kf.astype(BF16)
        vb = vf.astype(BF16)
        heads = [slice(h * SB_HEAD_DIM, (h + 1) * SB_HEAD_DIM) for h in range(SB_HEADS)]
        z = jnp.concatenate([lax.dot_general(q[:, hs], kb[:, hs], _NT, preferred_element_type=F32)
                             for hs in heads], axis=0)
        sp = jnp.maximum(z, 0.0) + jnp.log1p(jnp.exp(-jnp.abs(z)))
        l1m = -sp if mask is None else jnp.where(mask, -sp, 0.0)
        p0 = l1m.astype(BF16)
        r0 = l1m - p0.astype(F32)
        p1 = r0.astype(BF16)
        p2 = (r0 - p1.astype(F32)).astype(BF16)
        cs = (jnp.dot(p0, tri, preferred_element_type=F32)
              + jnp.dot(p1, tri, preferred_element_type=F32)
              + jnp.dot(p2, tri, preferred_element_type=F32))
        a = jnp.exp(z - sp + cs[:, :CHUNK] + carry)
        if mask is not None:
            a = jnp.where(mask, a, 0.0)
        ab = a.astype(BF16)
        for h, hs in enumerate(heads):
            oacc_ref[h] += jnp.dot(ab[h * CHUNK:(h + 1) * CHUNK, :], vb[:, hs], preferred_element_type=F32)
        return carry + cs[:, CHUNK:]

    @pl.when(i > 0)
    def _():
        start(i - 1, 0)

    oacc_ref[...] = jnp.zeros_like(oacc_ref)
    r = lax.broadcasted_iota(jnp.int32, (CHUNK, CHUNK), 0)
    c = lax.broadcasted_iota(jnp.int32, (CHUNK, CHUNK), 1)
    causal = jnp.concatenate([c < r] * SB_HEADS, axis=0)
    carry_ref[...] = block(kd_ref[...], vd_ref[...], jnp.zeros((SB_HEADS * CHUNK, CHUNK), F32), causal)

    def cond(st):
        jb, go, _ = st
        return jnp.logical_and(jb >= 0, go)

    def body(st):
        jb, _, slot = st
        wait(jb, slot)

        @pl.when(jb > 0)
        def _():
            start(jb - 1, 1 - slot)

        carry = block(kbuf[slot], vbuf[slot], carry_ref[...], None)
        carry_ref[...] = carry
        top = jnp.max(carry, axis=(0, 1), keepdims=True)
        return jb - 1, top[0, 0] > -SB_DONE, 1 - slot

    jb_end, _, slot_end = lax.while_loop(cond, body, (i - 1, True, 0))

    @pl.when(jb_end >= 0)
    def _():
        wait(jb_end, slot_end)

    for h in range(SB_HEADS):
        o_ref[:, h * SB_HEAD_DIM:(h + 1) * SB_HEAD_DIM] = oacc_ref[h]


def _stick_breaking(h, tri):
    s = h.shape[0]
    return pl.pallas_call(
        _attn_kernel,
        grid=(s // CHUNK,),
        in_specs=[pl.BlockSpec((CHUNK, SB_W), lambda i: (i, 0)),
                  pl.BlockSpec((CHUNK, SB_W), lambda i: (i, 1)),
                  pl.BlockSpec((CHUNK, SB_W), lambda i: (i, 2)),
                  pl.BlockSpec(memory_space=pl.ANY),
                  pl.BlockSpec((CHUNK, 2 * CHUNK), lambda i: (0, 0))],
        out_specs=pl.BlockSpec((CHUNK, SB_W), lambda i: (i, 0)),
        out_shape=jax.ShapeDtypeStruct((s, SB_W), F32),
        scratch_shapes=[pltpu.VMEM((2, CHUNK, SB_W), F32),
                        pltpu.VMEM((2, CHUNK, SB_W), F32),
                        pltpu.VMEM((SB_HEADS * CHUNK, CHUNK), F32),
                        pltpu.VMEM((SB_HEADS, CHUNK, SB_HEAD_DIM), F32),
                        pltpu.SemaphoreType.DMA((2, 2))],
        compiler_params=_params("arbitrary"),
    )(h, h, h, h, tri)


def _ret_kernel(q_ref, k_ref, v_ref, g_ref, cos_ref, sin_ref, dec_ref, zeta_ref, xi_ref, cd_ref,
                o_ref, state_ref):
    @pl.when(pl.program_id(1) == 0)
    def _():
        state_ref[...] = jnp.zeros_like(state_ref)

    half = RET_HEAD_DIM // 2
    for c in range(q_ref.shape[0] // CHUNK):
        sl = slice(c * CHUNK, (c + 1) * CHUNK)
        cs = cos_ref[sl, :]
        sn = sin_ref[sl, :]
        q = q_ref[sl, :]
        k = k_ref[sl, :]
        qf = q * cs + pltpu.roll(q, half, 1) * sn
        kf = (k * cs + pltpu.roll(k, half, 1) * sn) * (RET_HEAD_DIM ** -0.5)
        vb = v_ref[sl, :].astype(BF16)
        scores = lax.dot_general(qf.astype(BF16), kf.astype(BF16), _NT, preferred_element_type=F32) * dec_ref[0]
        st = state_ref[...]
        y = (jnp.dot(scores.astype(BF16), vb, preferred_element_type=F32)
             + jnp.dot((qf * xi_ref[0]).astype(BF16), st.astype(BF16), preferred_element_type=F32))
        kzt = (kf * zeta_ref[0]).T
        state_ref[...] = cd_ref[0] * st + jnp.dot(kzt.astype(BF16), vb, preferred_element_type=F32)
        mu = jnp.mean(y, axis=-1, keepdims=True)
        yc = y - mu
        var = jnp.mean(yc * yc, axis=-1, keepdims=True)
        g = g_ref[sl, :]
        o_ref[sl, :] = g * jax.nn.sigmoid(g) * (yc * lax.rsqrt(var + LN_EPS))


def _retention(h, consts, rows):
    s = h.shape[0]
    cos2, sin2, dec, zeta, xi, cd = consts
    base = 3 * SB_W // RET_HEAD_DIM

    def col(which):
        return lambda hh, t: (t, base + which * RET_HEADS + hh)

    head_const = pl.BlockSpec((1, CHUNK, CHUNK), lambda hh, t: (hh, 0, 0))
    return pl.pallas_call(
        _ret_kernel,
        grid=(RET_HEADS, s // rows),
        in_specs=[pl.BlockSpec((rows, RET_HEAD_DIM), col(0)),
                  pl.BlockSpec((rows, RET_HEAD_DIM), col(1)),
                  pl.BlockSpec((rows, RET_HEAD_DIM), col(2)),
                  pl.BlockSpec((rows, RET_HEAD_DIM), col(3)),
                  pl.BlockSpec((rows, RET_HEAD_DIM), lambda hh, t: (t, 0)),
                  pl.BlockSpec((rows, RET_HEAD_DIM), lambda hh, t: (t, 0)),
                  head_const, head_const, head_const, head_const],
        out_specs=pl.BlockSpec((rows, RET_HEAD_DIM), lambda hh, t: (t, hh)),
        out_shape=jax.ShapeDtypeStruct((s, RET_W), F32),
        scratch_shapes=[pltpu.VMEM((RET_HEAD_DIM, RET_HEAD_DIM), F32)],
        compiler_params=_params("parallel", "arbitrary"),
    )(h, h, h, h, cos2, sin2, dec, zeta, xi, cd)


def _retention_consts(s):
    half = RET_HEAD_DIM // 2
    inv = ROPE_BASE ** (-jnp.arange(half, dtype=F32) / half)
    ang = jnp.arange(s, dtype=F32)[:, None] * inv[None, :]
    cos, sin = jnp.cos(ang), jnp.sin(ang)
    cos2 = jnp.concatenate([cos, cos], axis=-1)
    sin2 = jnp.concatenate([-sin, sin], axis=-1)
    log_gamma = jnp.log1p(-jnp.exp2(-RET_DECAY_BASE - jnp.arange(RET_HEADS, dtype=F32)))
    idx = jnp.arange(CHUNK, dtype=F32)
    diff = idx[:, None] - idx[None, :]
    dec = jnp.where(diff >= 0, jnp.exp(log_gamma[:, None, None] * jnp.maximum(diff, 0.0)), 0.0)
    shape = (RET_HEADS, CHUNK, CHUNK)
    zeta = jnp.broadcast_to(jnp.exp(log_gamma[:, None] * (CHUNK - 1.0 - idx)[None, :])[:, :, None], shape)
    xi = jnp.broadcast_to(jnp.exp(log_gamma[:, None] * (idx + 1.0)[None, :])[:, :, None], shape)
    cd = jnp.broadcast_to(jnp.exp(log_gamma * CHUNK)[:, None, None], shape)
    return cos2, sin2, dec, zeta, xi, cd


def _sgu_kernel(u_ref, v_ref, g_ref, b_ref, w_ref, bs_ref, o_ref):
    r = lax.broadcasted_iota(jnp.int32, (CHUNK, CHUNK), 0)
    c = lax.broadcasted_iota(jnp.int32, (CHUNK, CHUNK), 1)
    causal = r >= c
    for grp in range(SGU_GROUPS):
        gs = slice(grp * SGU_GROUP_DIM, (grp + 1) * SGU_GROUP_DIM)
        w = jnp.where(causal, w_ref[grp], 0.0).astype(BF16)
        lg = g_ref[:, gs]
        lb = b_ref[:, gs]
        for ck in range(u_ref.shape[0] // CHUNK):
            sl = slice(ck * CHUNK, (ck + 1) * CHUNK)
            vg = _layernorm_rows(jax.nn.gelu(v_ref[sl, gs]), lg, lb)
            mixed = jnp.dot(w, vg.astype(BF16), preferred_element_type=F32) + bs_ref[grp]
            o_ref[sl, gs] = jax.nn.gelu(u_ref[sl, gs]) * mixed


def _sgu(h, ln_g, ln_b, w_s, bs_b, rows):
    s = h.shape[0]
    ublk = (3 * SB_W + 4 * RET_W) // SGU_W
    full = lambda shape: pl.BlockSpec(shape, lambda t: (0,) * len(shape))
    return pl.pallas_call(
        _sgu_kernel,
        grid=(s // rows,),
        in_specs=[pl.BlockSpec((rows, SGU_W), lambda t: (t, ublk)),
                  pl.BlockSpec((rows, SGU_W), lambda t: (t, ublk + 1)),
                  full((1, SGU_W)), full((1, SGU_W)),
                  full((SGU_GROUPS, CHUNK, CHUNK)), full((SGU_GROUPS, CHUNK, CHUNK))],
        out_specs=pl.BlockSpec((rows, SGU_W), lambda t: (t, 0)),
        out_shape=jax.ShapeDtypeStruct((s, SGU_W), F32),
        compiler_params=_params("parallel"),
    )(h, h, ln_g, ln_b, w_s, bs_b)


def _oproj_kernel(a_ref, r_ref, c_ref, x_ref, w_ref, g_ref, b_ref, xo_ref, xb_ref):
    mix = (jnp.dot(a_ref[...].astype(BF16), w_ref[0:SB_W, :], preferred_element_type=F32)
           + jnp.dot(r_ref[...].astype(BF16), w_ref[SB_W:SB_W + RET_W, :], preferred_element_type=F32)
           + jnp.dot(c_ref[...].astype(BF16), w_ref[SB_W + RET_W:, :], preferred_element_type=F32))
    y = _layernorm_rows(DN_ALPHA * x_ref[...] + mix, g_ref[...], b_ref[...])
    xo_ref[...] = y
    xb_ref[...] = y.astype(BF16)


def _oproj_ln(a, r, c, x, w, g, b, rows):
    s = x.shape[0]
    rowblk = lambda n: pl.BlockSpec((rows, n), lambda t: (t, 0))
    full = lambda shape: pl.BlockSpec(shape, lambda t: (0,) * len(shape))
    return pl.pallas_call(
        _oproj_kernel,
        grid=(s // rows,),
        in_specs=[rowblk(SB_W), rowblk(RET_W), rowblk(SGU_W), rowblk(D_MODEL),
                  full((D_MODEL, D_MODEL)), full((1, D_MODEL)), full((1, D_MODEL))],
        out_specs=[rowblk(D_MODEL), rowblk(D_MODEL)],
        out_shape=[jax.ShapeDtypeStruct((s, D_MODEL), F32), jax.ShapeDtypeStruct((s, D_MODEL), BF16)],
        compiler_params=_params("parallel"),
    )(a, r, c, x, w, g, b)


def _resln_kernel(x_ref, ft_ref, g_ref, b_ref, xo_ref, xb_ref):
    y = _layernorm_rows(DN_ALPHA * x_ref[...] + ft_ref[...].T, g_ref[...], b_ref[...])
    xo_ref[...] = y
    xb_ref[...] = y.astype(BF16)


def _res_ln(x, ft, g, b, rows):
    s = x.shape[0]
    rowblk = pl.BlockSpec((rows, D_MODEL), lambda t: (t, 0))
    colblk = pl.BlockSpec((D_MODEL, rows), lambda t: (0, t))
    full = pl.BlockSpec((1, D_MODEL), lambda t: (0, 0))
    return pl.pallas_call(
        _resln_kernel,
        grid=(s // rows,),
        in_specs=[rowblk, colblk, full, full],
        out_specs=[rowblk, rowblk],
        out_shape=[jax.ShapeDtypeStruct((s, D_MODEL), F32), jax.ShapeDtypeStruct((s, D_MODEL), BF16)],
        compiler_params=_params("parallel"),
    )(x, ft, g, b)


_CELLS = [(i, j) for i in range(PEER_TOPK) for j in range(PEER_TOPK) if (i + 1) * (j + 1) <= PEER_TOPK]


def _topk_kernel(xb_ref, wqt_ref, keys_ref, n_ref, e1_ref, r2_ref, e2_ref, s_scr, rank_scr, v_scr):
    tt = xb_ref.shape[0]
    qt = lax.dot_general(wqt_ref[...], xb_ref[...], _NT, preferred_element_type=F32)
    kidx = lax.broadcasted_iota(jnp.int32, (PEER_NKEYS, tt), 0).astype(F32)
    for hp in range(2 * PEER_HEADS):
        h, p = divmod(hp, 2)
        s = jnp.dot(keys_ref[p], qt[hp * PEER_SUBDIM:(hp + 1) * PEER_SUBDIM, :].astype(BF16),
                    preferred_element_type=F32)
        s_scr[hp] = s
        work = s
        rank = jnp.full((PEER_NKEYS, tt), float(PEER_TOPK), F32)
        for it in range(PEER_TOPK):
            m = jnp.max(work, axis=0, keepdims=True)
            first = jnp.min(jnp.where(work == m, kidx, float(PEER_NKEYS)), axis=0, keepdims=True)
            hit = kidx == first
            rank = jnp.where(hit, float(it), rank)
            work = jnp.where(hit, -jnp.inf, work)
            v_scr[p, it, h:h + 1, :] = m
        rank_scr[hp] = rank

    v1 = [v_scr[0, i] for i in range(PEER_TOPK)]
    v2 = [v_scr[1, j] for j in range(PEER_TOPK)]
    sums = {c: v1[c[0]] + v2[c[1]] for c in _CELLS}
    ahead = {c: jnp.full((PEER_HEADS, tt), float((c[0] + 1) * (c[1] + 1) - 1), F32) for c in _CELLS}
    for x, cx in enumerate(_CELLS):
        for cy in _CELLS[x + 1:]:
            if cy[1] >= cx[1]:
                continue
            first = sums[cx] >= sums[cy]
            ahead[cy] = ahead[cy] + jnp.where(first, 1.0, 0.0)
            ahead[cx] = ahead[cx] + jnp.where(first, 0.0, 1.0)
    ex1 = [jnp.exp(v - v1[0]) for v in v1]
    ex2 = [jnp.exp(v - v2[0]) for v in v2]
    cnt = [jnp.zeros((PEER_HEADS, tt), F32) for _ in range(PEER_TOPK)]
    zsum = jnp.zeros((PEER_HEADS, tt), F32)
    for c in _CELLS:
        sel = ahead[c] < float(PEER_TOPK)
        cnt[c[0]] = cnt[c[0]] + jnp.where(sel, 1.0, 0.0)
        zsum = zsum + jnp.where(sel, ex1[c[0]] * ex2[c[1]], 0.0)
    zinv = 1.0 / zsum

    for h in range(PEER_HEADS):
        r1 = rank_scr[2 * h]
        n = jnp.zeros((PEER_NKEYS, tt), F32)
        for i in range(PEER_TOPK):
            n = n + jnp.where(r1 == float(i), cnt[i][h:h + 1, :], 0.0)
        n_ref[h] = n
        e1_ref[h] = jnp.exp(s_scr[2 * h] - v1[0][h:h + 1, :]) * zinv[h:h + 1, :]
        r2_ref[h] = rank_scr[2 * h + 1].astype(BF16)
        e2_ref[h] = jnp.exp(s_scr[2 * h + 1] - v2[0][h:h + 1, :]).astype(BF16)


def _peer_topk(xb, wqt, keys, tt):
    s = xb.shape[0]
    out = jax.ShapeDtypeStruct((PEER_HEADS, PEER_NKEYS, s), F32)
    outb = jax.ShapeDtypeStruct((PEER_HEADS, PEER_NKEYS, s), BF16)
    oblk = pl.BlockSpec((PEER_HEADS, PEER_NKEYS, tt), lambda t: (0, 0, t))
    return pl.pallas_call(
        _topk_kernel,
        grid=(s // tt,),
        in_specs=[pl.BlockSpec((tt, D_MODEL), lambda t: (t, 0)),
                  pl.BlockSpec((2 * PEER_HEADS * PEER_SUBDIM, D_MODEL), lambda t: (0, 0)),
                  pl.BlockSpec((2, PEER_NKEYS, PEER_SUBDIM), lambda t: (0, 0, 0))],
        out_specs=[oblk, oblk, oblk, oblk],
        out_shape=[out, out, outb, outb],
        scratch_shapes=[pltpu.VMEM((2 * PEER_HEADS, PEER_NKEYS, tt), F32),
                        pltpu.VMEM((2 * PEER_HEADS, PEER_NKEYS, tt), F32),
                        pltpu.VMEM((2, PEER_TOPK, PEER_HEADS, tt), F32)],
        compiler_params=_params("parallel"),
    )(xb, wqt, keys)


def _dense_kernel(xb_ref, u_ref, vt_ref, n_ref, e1_ref, r2_ref, e2_ref, o_ref):
    @pl.when(pl.program_id(1) == 0)
    def _():
        o_ref[...] = jnp.zeros_like(o_ref)

    for th in range(xb_ref.shape[0] // DENSE_HALF):
        ts = slice(th * DENSE_HALF, (th + 1) * DENSE_HALF)
        hid = lax.dot_general(u_ref[...], xb_ref[ts, :], _NT, preferred_element_type=F32)
        pieces = []
        for a in range(u_ref.shape[0] // PEER_NKEYS):
            gate = jnp.zeros((PEER_NKEYS, DENSE_HALF), BF16)
            for h in range(PEER_HEADS):
                n_row = n_ref[h, a:a + 1, ts].astype(BF16)
                e1_row = e1_ref[h, a:a + 1, ts].astype(BF16)
                gate = gate + jnp.where(r2_ref[h, :, ts] < n_row, e2_ref[h, :, ts] * e1_row, jnp.zeros((), BF16))
            rows = slice(a * PEER_NKEYS, (a + 1) * PEER_NKEYS)
            pieces.append(gate * jax.nn.gelu(hid[rows, :]).astype(BF16))
        coef = jnp.concatenate(pieces, axis=0)
        o_ref[:, ts] += jnp.dot(vt_ref[...], coef, preferred_element_type=F32)


def _peer_dense(xb, u, vt, n, e1, r2, e2, tt, eb):
    s = xb.shape[0]
    ablk = eb // PEER_NKEYS
    per_a = pl.BlockSpec((PEER_HEADS, ablk, tt), lambda t, j: (0, j, t))
    per_b = pl.BlockSpec((PEER_HEADS, PEER_NKEYS, tt), lambda t, j: (0, 0, t))
    return pl.pallas_call(
        _dense_kernel,
        grid=(s // tt, PEER_EXPERTS // eb),
        in_specs=[pl.BlockSpec((tt, D_MODEL), lambda t, j: (t, 0)),
                  pl.BlockSpec((eb, D_MODEL), lambda t, j: (j, 0)),
                  pl.BlockSpec((D_MODEL, eb), lambda t, j: (0, j)),
                  per_a, per_a, per_b, per_b],
        out_specs=pl.BlockSpec((D_MODEL, tt), lambda t, j: (0, t)),
        out_shape=jax.ShapeDtypeStruct((D_MODEL, s), F32),
        compiler_params=_params("parallel", "arbitrary"),
    )(xb, u, vt, n, e1, r2, e2)


def _tri_matrix():
    j = np.arange(CHUNK)[:, None]
    s = np.arange(2 * CHUNK)[None, :]
    return jnp.asarray(np.where((s >= CHUNK) | (j > s), 1.0, 0.0), dtype=BF16)


def kernel(x, w_in, w_out, sgu_ln_g, sgu_ln_b, sgu_w, sgu_b, ln1_g, ln1_b, peer_wq, peer_sub_keys,
           peer_u, peer_v, ln2_g, ln2_b):
    bsz, s, d = x.shape
    assert bsz == 1 and d == D_MODEL and s % 512 == 0
    tm = min(1024, s)
    tri = _tri_matrix()
    ret_consts = _retention_consts(s)

    xf = x.reshape(s, d)
    xb = xf.astype(BF16)
    for l in range(DEPTH):
        h = _matmul(xb, w_in[l].astype(BF16), tm, 512)
        a_out = _stick_breaking(h, tri)
        r_out = _retention(h, ret_consts, 512)
        bs_b = jnp.broadcast_to(sgu_b[l][:, :, None], (SGU_GROUPS, CHUNK, CHUNK))
        c_out = _sgu(h, sgu_ln_g[l][None, :], sgu_ln_b[l][None, :], sgu_w[l], bs_b, 256)
        xf, xb = _oproj_ln(a_out, r_out, c_out, xf, w_out[l].astype(BF16),
                           ln1_g[l][None, :], ln1_b[l][None, :], 256)
        n, e1, r2, e2 = _peer_topk(xb, peer_wq[l].T.astype(BF16), peer_sub_keys[l].astype(BF16), 256)
        ffn_t = _peer_dense(xb, peer_u[l].astype(BF16), peer_v[l].T.astype(BF16), n, e1, r2, e2,
                            min(1024, s), 1024)
        xf, xb = _res_ln(xf, ffn_t, ln2_g[l][None, :], ln2_b[l][None, :], 512)
    return xf.reshape(bsz, s, d)
```

```python
import numpy as np
import jax
import jax.numpy as jnp
from jax import lax
from jax.experimental import pallas as pl
from jax.experimental.pallas import tpu as pltpu

D_MODEL = 2048
DEPTH = 2

SB_HEADS = 8
SB_HEAD_DIM = 64
RET_HEADS = 8
RET_HEAD_DIM = 128
SGU_GROUPS = 4
SGU_GROUP_DIM = 128
SB_W = SB_HEADS * SB_HEAD_DIM
RET_W = RET_HEADS * RET_HEAD_DIM
SGU_W = SGU_GROUPS * SGU_GROUP_DIM
IN_COLS = 3 * SB_W + 4 * RET_W + 2 * SGU_W

CHUNK = 128
RET_DECAY_BASE = 5.0
ROPE_BASE = 10000.0

PEER_HEADS = 8
PEER_NKEYS = 128
PEER_EXPERTS = PEER_NKEYS * PEER_NKEYS
PEER_TOPK = 16
PEER_SUBDIM = 128

DN_ALPHA = (2.0 * DEPTH) ** 0.25
LN_EPS = 1e-5

SB_DONE = 104.0

VMEM_LIMIT = 56 * 1024 * 1024
DENSE_HALF = 512
RET_GROUP = 4

F32 = jnp.float32
BF16 = jnp.bfloat16

_NT = (((1,), (1,)), ((), ()))


def _params(*sem):
    return pltpu.CompilerParams(dimension_semantics=sem, vmem_limit_bytes=VMEM_LIMIT)


def _layernorm_rows(y, g, b):
    mu = jnp.mean(y, axis=-1, keepdims=True)
    yc = y - mu
    var = jnp.mean(yc * yc, axis=-1, keepdims=True)
    return yc * lax.rsqrt(var + LN_EPS) * g + b


def _mm_kernel(a_ref, b_ref, o_ref):
    o_ref[...] = jnp.dot(a_ref[...], b_ref[...], preferred_element_type=F32)


def _matmul(a, b, tm, tn):
    m, k = a.shape
    n = b.shape[1]
    return pl.pallas_call(
        _mm_kernel,
        grid=(m // tm, n // tn),
        in_specs=[pl.BlockSpec((tm, k), lambda i, j: (i, 0)),
                  pl.BlockSpec((k, tn), lambda i, j: (0, j))],
        out_specs=pl.BlockSpec((tm, tn), lambda i, j: (i, j)),
        out_shape=jax.ShapeDtypeStruct((m, n), F32),
        compiler_params=_params("parallel", "parallel"),
    )(a, b)


def _attn_kernel(q_ref, kd_ref, vd_ref, h_any, tri_ref, o_ref, kbuf, vbuf, carry_ref, oacc_ref, sem):
    i = pl.program_id(0)
    q = (q_ref[...] * (SB_HEAD_DIM ** -0.5)).astype(BF16)
    tri = tri_ref[...]

    def key_copies(jb, slot):
        r0 = pl.multiple_of(jb * CHUNK, CHUNK)
        ck = pltpu.make_async_copy(h_any.at[pl.ds(r0, CHUNK), pl.ds(SB_W, SB_W)], kbuf.at[slot], sem.at[0, slot])
        cv = pltpu.make_async_copy(h_any.at[pl.ds(r0, CHUNK), pl.ds(2 * SB_W, SB_W)], vbuf.at[slot], sem.at[1, slot])
        return ck, cv

    def start(jb, slot):
        for c in key_copies(jb, slot):
            c.start()

    def wait(jb, slot):
        for c in key_copies(jb, slot):
            c.wait()

    def block(kf, vf, carry, mask):
        kb = kf.astype(BF16)
        vb = vf.astype(BF16)
        heads = [slice(h * SB_HEAD_DIM, (h + 1) * SB_HEAD_DIM) for h in range(SB_HEADS)]
        z = jnp.concatenate([lax.dot_general(q[:, hs], kb[:, hs], _NT, preferred_element_type=F32)
                             for hs in heads], axis=0)
        sp = jnp.maximum(z, 0.0) + jnp.log1p(jnp.exp(-jnp.abs(z)))
        l1m = -sp if mask is None else jnp.where(mask, -sp, 0.0)
        p0 = l1m.astype(BF16)
        r0 = l1m - p0.astype(F32)
        p1 = r0.astype(BF16)
        p2 = (r0 - p1.astype(F32)).astype(BF16)
        cs = (jnp.dot(p0, tri, preferred_element_type=F32)
              + jnp.dot(p1, tri, preferred_element_type=F32)
              + jnp.dot(p2, tri, preferred_element_type=F32))
        a = jnp.exp(z - sp + cs[:, :CHUNK] + carry)
        if mask is not None:
            a = jnp.where(mask, a, 0.0)
        ab = a.astype(BF16)
        for h, hs in enumerate(heads):
            oacc_ref[h] += jnp.dot(ab[h * CHUNK:(h + 1) * CHUNK, :], vb[:, hs], preferred_element_type=F32)
        return carry + cs[:, CHUNK:]

    @pl.when(i > 0)
    def _():
        start(i - 1, 0)

    oacc_ref[...] = jnp.zeros_like(oacc_ref)
    r = lax.broadcasted_iota(jnp.int32, (CHUNK, CHUNK), 0)
    c = lax.broadcasted_iota(jnp.int32, (CHUNK, CHUNK), 1)
    causal = jnp.concatenate([c < r] * SB_HEADS, axis=0)
    carry_ref[...] = block(kd_ref[...], vd_ref[...], jnp.zeros((SB_HEADS * CHUNK, CHUNK), F32), causal)

    def cond(st):
        jb, go, _ = st
        return jnp.logical_and(jb >= 0, go)

    def body(st):
        jb, _, slot = st
        wait(jb, slot)

        @pl.when(jb > 0)
        def _():
            start(jb - 1, 1 - slot)

        carry = block(kbuf[slot], vbuf[slot], carry_ref[...], None)
        carry_ref[...] = carry
        top = jnp.max(carry, axis=(0, 1), keepdims=True)
        return jb - 1, top[0, 0] > -SB_DONE, 1 - slot

    jb_end, _, slot_end = lax.while_loop(cond, body, (i - 1, True, 0))

    @pl.when(jb_end >= 0)
    def _():
        wait(jb_end, slot_end)

    for h in range(SB_HEADS):
        o_ref[:, h * SB_HEAD_DIM:(h + 1) * SB_HEAD_DIM] = oacc_ref[h]


def _stick_breaking(h, tri):
    s = h.shape[0]
    return pl.pallas_call(
        _attn_kernel,
        grid=(s // CHUNK,),
        in_specs=[pl.BlockSpec((CHUNK, SB_W), lambda i: (i, 0)),
                  pl.BlockSpec((CHUNK, SB_W), lambda i: (i, 1)),
                  pl.BlockSpec((CHUNK, SB_W), lambda i: (i, 2)),
                  pl.BlockSpec(memory_space=pl.ANY),
                  pl.BlockSpec((CHUNK, 2 * CHUNK), lambda i: (0, 0))],
        out_specs=pl.BlockSpec((CHUNK, SB_W), lambda i: (i, 0)),
        out_shape=jax.ShapeDtypeStruct((s, SB_W), F32),
        scratch_shapes=[pltpu.VMEM((2, CHUNK, SB_W), F32),
                        pltpu.VMEM((2, CHUNK, SB_W), F32),
                        pltpu.VMEM((SB_HEADS * CHUNK, CHUNK), F32),
                        pltpu.VMEM((SB_HEADS, CHUNK, SB_HEAD_DIM), F32),
                        pltpu.SemaphoreType.DMA((2, 2))],
        compiler_params=_params("arbitrary"),
    )(h, h, h, h, tri)


def _ret_kernel(q_ref, k_ref, v_ref, g_ref, cos_ref, sin_ref, dec_ref, zeta_ref, xi_ref, cd_ref,
                o_ref, state_ref):
    @pl.when(pl.program_id(1) == 0)
    def _():
        state_ref[...] = jnp.zeros_like(state_ref)

    half = RET_HEAD_DIM // 2
    for c in range(q_ref.shape[0] // CHUNK):
        sl = slice(c * CHUNK, (c + 1) * CHUNK)
        cs = cos_ref[sl, :]
        sn = sin_ref[sl, :]
        for hd in range(RET_GROUP):
            hs = slice(hd * RET_HEAD_DIM, (hd + 1) * RET_HEAD_DIM)
            q = q_ref[sl, hs]
            k = k_ref[sl, hs]
            qf = q * cs + pltpu.roll(q, half, 1) * sn
            kf = (k * cs + pltpu.roll(k, half, 1) * sn) * (RET_HEAD_DIM ** -0.5)
            vb = v_ref[sl, hs].astype(BF16)
            scores = (lax.dot_general(qf.astype(BF16), kf.astype(BF16), _NT, preferred_element_type=F32)
                      * dec_ref[hd])
            st = state_ref[hd]
            y = (jnp.dot(scores.astype(BF16), vb, preferred_element_type=F32)
                 + jnp.dot((qf * xi_ref[hd]).astype(BF16), st.astype(BF16), preferred_element_type=F32))
            kzt = (kf * zeta_ref[hd]).T
            state_ref[hd] = cd_ref[hd] * st + jnp.dot(kzt.astype(BF16), vb, preferred_element_type=F32)
            mu = jnp.mean(y, axis=-1, keepdims=True)
            yc = y - mu
            var = jnp.mean(yc * yc, axis=-1, keepdims=True)
            g = g_ref[sl, hs]
            o_ref[sl, hs] = g * jax.nn.sigmoid(g) * (yc * lax.rsqrt(var + LN_EPS))


def _retention(h, consts, rows):
    s = h.shape[0]
    cos2, sin2, dec, zeta, xi, cd = consts
    width = RET_GROUP * RET_HEAD_DIM
    base = 3 * SB_W // width

    def col(which):
        return lambda grp, t: (t, base + which * (RET_HEADS // RET_GROUP) + grp)

    head_const = pl.BlockSpec((RET_GROUP, CHUNK, CHUNK), lambda grp, t: (grp, 0, 0))
    return pl.pallas_call(
        _ret_kernel,
        grid=(RET_HEADS // RET_GROUP, s // rows),
        in_specs=[pl.BlockSpec((rows, width), col(0)),
                  pl.BlockSpec((rows, width), col(1)),
                  pl.BlockSpec((rows, width), col(2)),
                  pl.BlockSpec((rows, width), col(3)),
                  pl.BlockSpec((rows, RET_HEAD_DIM), lambda grp, t: (t, 0)),
                  pl.BlockSpec((rows, RET_HEAD_DIM), lambda grp, t: (t, 0)),
                  head_const, head_const, head_const, head_const],
        out_specs=pl.BlockSpec((rows, width), lambda grp, t: (t, grp)),
        out_shape=jax.ShapeDtypeStruct((s, RET_W), F32),
        scratch_shapes=[pltpu.VMEM((RET_GROUP, RET_HEAD_DIM, RET_HEAD_DIM), F32)],
        compiler_params=_params("parallel", "arbitrary"),
    )(h, h, h, h, cos2, sin2, dec, zeta, xi, cd)


def _retention_consts(s):
    half = RET_HEAD_DIM // 2
    inv = ROPE_BASE ** (-jnp.arange(half, dtype=F32) / half)
    ang = jnp.arange(s, dtype=F32)[:, None] * inv[None, :]
    cos, sin = jnp.cos(ang), jnp.sin(ang)
    cos2 = jnp.concatenate([cos, cos], axis=-1)
    sin2 = jnp.concatenate([-sin, sin], axis=-1)
    log_gamma = jnp.log1p(-jnp.exp2(-RET_DECAY_BASE - jnp.arange(RET_HEADS, dtype=F32)))
    idx = jnp.arange(CHUNK, dtype=F32)
    diff = idx[:, None] - idx[None, :]
    dec = jnp.where(diff >= 0, jnp.exp(log_gamma[:, None, None] * jnp.maximum(diff, 0.0)), 0.0)
    shape = (RET_HEADS, CHUNK, CHUNK)
    zeta = jnp.broadcast_to(jnp.exp(log_gamma[:, None] * (CHUNK - 1.0 - idx)[None, :])[:, :, None], shape)
    xi = jnp.broadcast_to(jnp.exp(log_gamma[:, None] * (idx + 1.0)[None, :])[:, :, None], shape)
    cd = jnp.broadcast_to(jnp.exp(log_gamma * CHUNK)[:, None, None], shape)
    return cos2, sin2, dec, zeta, xi, cd


def _sgu_kernel(u_ref, v_ref, g_ref, b_ref, w_ref, bs_ref, o_ref):
    r = lax.broadcasted_iota(jnp.int32, (CHUNK, CHUNK), 0)
    c = lax.broadcasted_iota(jnp.int32, (CHUNK, CHUNK), 1)
    causal = r >= c
    for grp in range(SGU_GROUPS):
        gs = slice(grp * SGU_GROUP_DIM, (grp + 1) * SGU_GROUP_DIM)
        w = jnp.where(causal, w_ref[grp], 0.0).astype(BF16)
        lg = g_ref[:, gs]
        lb = b_ref[:, gs]
        for ck in range(u_ref.shape[0] // CHUNK):
            sl = slice(ck * CHUNK, (ck + 1) * CHUNK)
            vg = _layernorm_rows(jax.nn.gelu(v_ref[sl, gs]), lg, lb)
            mixed = jnp.dot(w, vg.astype(BF16), preferred_element_type=F32) + bs_ref[grp]
            o_ref[sl, gs] = jax.nn.gelu(u_ref[sl, gs]) * mixed


def _sgu(h, ln_g, ln_b, w_s, bs_b, rows):
    s = h.shape[0]
    ublk = (3 * SB_W + 4 * RET_W) // SGU_W
    full = lambda shape: pl.BlockSpec(shape, lambda t: (0,) * len(shape))
    return pl.pallas_call(
        _sgu_kernel,
        grid=(s // rows,),
        in_specs=[pl.BlockSpec((rows, SGU_W), lambda t: (t, ublk)),
                  pl.BlockSpec((rows, SGU_W), lambda t: (t, ublk + 1)),
                  full((1, SGU_W)), full((1, SGU_W)),
                  full((SGU_GROUPS, CHUNK, CHUNK)), full((SGU_GROUPS, CHUNK, CHUNK))],
        out_specs=pl.BlockSpec((rows, SGU_W), lambda t: (t, 0)),
        out_shape=jax.ShapeDtypeStruct((s, SGU_W), F32),
        compiler_params=_params("parallel"),
    )(h, h, ln_g, ln_b, w_s, bs_b)


def _oproj_kernel(a_ref, r_ref, c_ref, x_ref, w_ref, g_ref, b_ref, xo_ref, xt_ref):
    mix = (jnp.dot(a_ref[...].astype(BF16), w_ref[0:SB_W, :], preferred_element_type=F32)
           + jnp.dot(r_ref[...].astype(BF16), w_ref[SB_W:SB_W + RET_W, :], preferred_element_type=F32)
           + jnp.dot(c_ref[...].astype(BF16), w_ref[SB_W + RET_W:, :], preferred_element_type=F32))
    y = _layernorm_rows(DN_ALPHA * x_ref[...] + mix, g_ref[...], b_ref[...])
    xo_ref[...] = y
    xt_ref[...] = y.T.astype(BF16)


def _oproj_ln(a, r, c, x, w, g, b, rows):
    s = x.shape[0]
    rowblk = lambda n: pl.BlockSpec((rows, n), lambda t: (t, 0))
    full = lambda shape: pl.BlockSpec(shape, lambda t: (0,) * len(shape))
    return pl.pallas_call(
        _oproj_kernel,
        grid=(s // rows,),
        in_specs=[rowblk(SB_W), rowblk(RET_W), rowblk(SGU_W), rowblk(D_MODEL),
                  full((D_MODEL, D_MODEL)), full((1, D_MODEL)), full((1, D_MODEL))],
        out_specs=[rowblk(D_MODEL), pl.BlockSpec((D_MODEL, rows), lambda t: (0, t))],
        out_shape=[jax.ShapeDtypeStruct((s, D_MODEL), F32), jax.ShapeDtypeStruct((D_MODEL, s), BF16)],
        compiler_params=_params("parallel"),
    )(a, r, c, x, w, g, b)


def _resln_kernel(x_ref, ft_ref, g_ref, b_ref, xo_ref, xb_ref):
    y = _layernorm_rows(DN_ALPHA * x_ref[...] + ft_ref[...].T, g_ref[...], b_ref[...])
    xo_ref[...] = y
    xb_ref[...] = y.astype(BF16)


def _res_ln(x, ft, g, b, rows):
    s = x.shape[0]
    rowblk = pl.BlockSpec((rows, D_MODEL), lambda t: (t, 0))
    colblk = pl.BlockSpec((D_MODEL, rows), lambda t: (0, t))
    full = pl.BlockSpec((1, D_MODEL), lambda t: (0, 0))
    return pl.pallas_call(
        _resln_kernel,
        grid=(s // rows,),
        in_specs=[rowblk, colblk, full, full],
        out_specs=[rowblk, rowblk],
        out_shape=[jax.ShapeDtypeStruct((s, D_MODEL), F32), jax.ShapeDtypeStruct((s, D_MODEL), BF16)],
        compiler_params=_params("parallel"),
    )(x, ft, g, b)


_CELLS = [(i, j) for i in range(PEER_TOPK) for j in range(PEER_TOPK) if (i + 1) * (j + 1) <= PEER_TOPK]


def _topk_kernel(xt_ref, wqt_ref, keys_ref, n_ref, e1_ref, r2_ref, e2_ref, s_scr, rank_scr, v_scr):
    tt = xt_ref.shape[1]
    qt = jnp.dot(wqt_ref[...], xt_ref[...], preferred_element_type=F32)
    kidx = lax.broadcasted_iota(jnp.int32, (PEER_NKEYS, tt), 0).astype(F32)

    def extract(hp, exact):
        h, p = divmod(hp, 2)
        work = s_scr[hp]
        rank = jnp.full((PEER_NKEYS, tt), float(PEER_TOPK), F32)
        for it in range(PEER_TOPK):
            m = jnp.max(work, axis=0, keepdims=True)
            hit = work == m
            if exact:
                first = jnp.min(jnp.where(hit, kidx, float(PEER_NKEYS)), axis=0, keepdims=True)
                hit = kidx == first
            rank = jnp.where(hit, float(it), rank)
            work = jnp.where(hit, -jnp.inf, work)
            v_scr[p, it, h:h + 1, :] = m
        rank_scr[hp] = rank
        return jnp.sum(jnp.where(rank < float(PEER_TOPK), 1.0, 0.0), axis=0, keepdims=True)

    most = jnp.zeros((1, tt), F32)
    for hp in range(2 * PEER_HEADS):
        s_scr[hp] = jnp.dot(keys_ref[hp % 2], qt[hp * PEER_SUBDIM:(hp + 1) * PEER_SUBDIM, :].astype(BF16),
                            preferred_element_type=F32)
        most = jnp.maximum(most, extract(hp, exact=False))

    @pl.when(jnp.max(most, axis=(0, 1), keepdims=True)[0, 0] > float(PEER_TOPK))
    def _():
        for hp in range(2 * PEER_HEADS):
            extract(hp, exact=True)

    v1 = [v_scr[0, i] for i in range(PEER_TOPK)]
    v2 = [v_scr[1, j] for j in range(PEER_TOPK)]
    sums = {c: v1[c[0]] + v2[c[1]] for c in _CELLS}
    ahead = {c: jnp.full((PEER_HEADS, tt), float((c[0] + 1) * (c[1] + 1) - 1), F32) for c in _CELLS}
    for x, cx in enumerate(_CELLS):
        for cy in _CELLS[x + 1:]:
            if cy[1] >= cx[1]:
                continue
            first = sums[cx] >= sums[cy]
            ahead[cy] = ahead[cy] + jnp.where(first, 1.0, 0.0)
            ahead[cx] = ahead[cx] + jnp.where(first, 0.0, 1.0)
    ex1 = [jnp.exp(v - v1[0]) for v in v1]
    ex2 = [jnp.exp(v - v2[0]) for v in v2]
    cnt = [jnp.zeros((PEER_HEADS, tt), F32) for _ in range(PEER_TOPK)]
    zsum = jnp.zeros((PEER_HEADS, tt), F32)
    for c in _CELLS:
        sel = ahead[c] < float(PEER_TOPK)
        cnt[c[0]] = cnt[c[0]] + jnp.where(sel, 1.0, 0.0)
        zsum = zsum + jnp.where(sel, ex1[c[0]] * ex2[c[1]], 0.0)
    zinv = 1.0 / zsum

    for h in range(PEER_HEADS):
        r1 = rank_scr[2 * h]
        n = jnp.zeros((PEER_NKEYS, tt), F32)
        for i in range(PEER_TOPK):
            n = n + jnp.where(r1 == float(i), cnt[i][h:h + 1, :], 0.0)
        n_ref[h] = n
        e1_ref[h] = jnp.exp(s_scr[2 * h] - v1[0][h:h + 1, :]) * zinv[h:h + 1, :]
        r2_ref[h] = rank_scr[2 * h + 1].astype(BF16)
        e2_ref[h] = jnp.exp(s_scr[2 * h + 1] - v2[0][h:h + 1, :]).astype(BF16)


def _peer_topk(xt, wqt, keys, tt):
    s = xt.shape[1]
    out = jax.ShapeDtypeStruct((PEER_HEADS, PEER_NKEYS, s), F32)
    outb = jax.ShapeDtypeStruct((PEER_HEADS, PEER_NKEYS, s), BF16)
    oblk = pl.BlockSpec((PEER_HEADS, PEER_NKEYS, tt), lambda t: (0, 0, t))
    return pl.pallas_call(
        _topk_kernel,
        grid=(s // tt,),
        in_specs=[pl.BlockSpec((D_MODEL, tt), lambda t: (0, t)),
                  pl.BlockSpec((2 * PEER_HEADS * PEER_SUBDIM, D_MODEL), lambda t: (0, 0)),
                  pl.BlockSpec((2, PEER_NKEYS, PEER_SUBDIM), lambda t: (0, 0, 0))],
        out_specs=[oblk, oblk, oblk, oblk],
        out_shape=[out, out, outb, outb],
        scratch_shapes=[pltpu.VMEM((2 * PEER_HEADS, PEER_NKEYS, tt), F32),
                        pltpu.VMEM((2 * PEER_HEADS, PEER_NKEYS, tt), F32),
                        pltpu.VMEM((2, PEER_TOPK, PEER_HEADS, tt), F32)],
        compiler_params=_params("parallel"),
    )(xt, wqt, keys)


def _bcast_rows_bf16(row):
    tile = jnp.broadcast_to(row, (16, row.shape[1])).astype(BF16)
    return jnp.concatenate([tile] * (PEER_NKEYS // 16), axis=0)


def _dense_kernel(xt_ref, u_ref, vt_ref, n_ref, e1_ref, r2_ref, e2_ref, o_ref, gate_scr, hid_scr, coef_scr):
    @pl.when(pl.program_id(1) == 0)
    def _():
        o_ref[...] = jnp.zeros_like(o_ref)

    halves = xt_ref.shape[1] // DENSE_HALF
    for th in range(halves):
        ts = slice(th * DENSE_HALF, (th + 1) * DENSE_HALF)
        for a in range(u_ref.shape[0] // PEER_NKEYS):
            gate = jnp.zeros((PEER_NKEYS, DENSE_HALF), BF16)
            for h in range(PEER_HEADS):
                n_row = _bcast_rows_bf16(n_ref[h, a:a + 1, ts])
                e1_row = _bcast_rows_bf16(e1_ref[h, a:a + 1, ts])
                gate = gate + jnp.where(r2_ref[h, :, ts] < n_row, e2_ref[h, :, ts] * e1_row, jnp.zeros((), BF16))
            gate_scr[th, a * PEER_NKEYS:(a + 1) * PEER_NKEYS, :] = gate
    for th in range(halves):
        ts = slice(th * DENSE_HALF, (th + 1) * DENSE_HALF)
        hid_scr[th] = jnp.dot(u_ref[...], xt_ref[:, ts], preferred_element_type=F32)
    for th in range(halves):
        coef_scr[th] = gate_scr[th] * jax.nn.gelu(hid_scr[th].astype(BF16))
    for th in range(halves):
        ts = slice(th * DENSE_HALF, (th + 1) * DENSE_HALF)
        o_ref[:, ts] += jnp.dot(vt_ref[...], coef_scr[th], preferred_element_type=F32)


def _peer_dense(xt, u, vt, n, e1, r2, e2, tt, eb):
    s = xt.shape[1]
    ablk = eb // PEER_NKEYS
    halves = tt // DENSE_HALF
    once = pl.Buffered(1)
    per_a = pl.BlockSpec((PEER_HEADS, ablk, tt), lambda t, j: (0, j, t))
    per_b = pl.BlockSpec((PEER_HEADS, PEER_NKEYS, tt), lambda t, j: (0, 0, t), pipeline_mode=once)
    return pl.pallas_call(
        _dense_kernel,
        grid=(s // tt, PEER_EXPERTS // eb),
        in_specs=[pl.BlockSpec((D_MODEL, tt), lambda t, j: (0, t), pipeline_mode=once),
                  pl.BlockSpec((eb, D_MODEL), lambda t, j: (j, 0)),
                  pl.BlockSpec((D_MODEL, eb), lambda t, j: (0, j)),
                  per_a, per_a, per_b, per_b],
        out_specs=pl.BlockSpec((D_MODEL, tt), lambda t, j: (0, t), pipeline_mode=once),
        out_shape=jax.ShapeDtypeStruct((D_MODEL, s), F32),
        scratch_shapes=[pltpu.VMEM((halves, eb, DENSE_HALF), BF16),
                        pltpu.VMEM((halves, eb, DENSE_HALF), F32),
                        pltpu.VMEM((halves, eb, DENSE_HALF), BF16)],
        compiler_params=_params("parallel", "arbitrary"),
    )(xt, u, vt, n, e1, r2, e2)


def _tri_matrix():
    j = np.arange(CHUNK)[:, None]
    s = np.arange(2 * CHUNK)[None, :]
    return jnp.asarray(np.where((s >= CHUNK) | (j > s), 1.0, 0.0), dtype=BF16)


def kernel(x, w_in, w_out, sgu_ln_g, sgu_ln_b, sgu_w, sgu_b, ln1_g, ln1_b, peer_wq, peer_sub_keys,
           peer_u, peer_v, ln2_g, ln2_b):
    bsz, s, d = x.shape
    assert bsz == 1 and d == D_MODEL and s % 512 == 0
    tm = min(1024, s)
    tri = _tri_matrix()
    ret_consts = _retention_consts(s)

    xf = x.reshape(s, d)
    xb = xf.astype(BF16)
    for l in range(DEPTH):
        h = _matmul(xb, w_in[l].astype(BF16), tm, 512)
        a_out = _stick_breaking(h, tri)
        r_out = _retention(h, ret_consts, 512)
        bs_b = jnp.broadcast_to(sgu_b[l][:, :, None], (SGU_GROUPS, CHUNK, CHUNK))
        c_out = _sgu(h, sgu_ln_g[l][None, :], sgu_ln_b[l][None, :], sgu_w[l], bs_b, 256)
        xf, xt = _oproj_ln(a_out, r_out, c_out, xf, w_out[l].astype(BF16),
                           ln1_g[l][None, :], ln1_b[l][None, :], 256)
        n, e1, r2, e2 = _peer_topk(xt, peer_wq[l].T.astype(BF16), peer_sub_keys[l].astype(BF16), 256)
        ffn_t = _peer_dense(xt, peer_u[l].astype(BF16), peer_v[l].T.astype(BF16), n, e1, r2, e2,
                            min(1024, s), 1024)
        xf, xb = _res_ln(xf, ffn_t, ln2_g[l][None, :], ln2_b[l][None, :], 512)
    return xf.reshape(bsz, s, d)
```
